```python
import math
import jax, jax.numpy as jnp
from jax import lax
import numpy as np

D_MODEL = 2048
BATCH = 1
SEQ = 8192
DEPTH = 1
DEC_BATCH = 32
DEC_SEQ = 1
PAST_LEN = 8192
PAGE_SIZE = 128

W_CONV = D_MODEL // 2
CONV_K = 3
N_HEADS = 8
HEAD_DIM = D_MODEL // (4 * N_HEADS)
V_DIM = 2 * HEAD_DIM
ATTN_W = N_HEADS * 2 * HEAD_DIM
Q_BLOCK = 128
N_GROUPS = 4
EXPERTS_PER_GROUP = 8
N_EXPERTS = N_GROUPS * EXPERTS_PER_GROUP
TOP_K = 2
D_EXPERT = D_MODEL // 4
NORM_EPS = 1e-6
IN_SPLITS = (W_CONV, W_CONV, W_CONV, ATTN_W, ATTN_W, ATTN_W, 2 * D_MODEL)
IN_COLS = 3 * W_CONV + 3 * ATTN_W + 2 * D_MODEL

kernel_name = 'hybrid_conv_diffattn_hiermoe_step'


def _rms(x, g):
    xf = x.astype(jnp.float32)
    y = xf * lax.rsqrt(jnp.mean(xf * xf, axis=-1, keepdims=True) + NORM_EPS)
    return (y * g.astype(jnp.float32)).astype(x.dtype)


def _alibi_slopes():
    return jnp.exp2(-8.0 * jnp.arange(1, N_HEADS + 1, dtype=jnp.float32) / N_HEADS)


def _diff_attn_block(q, k, v, q_pos, k_pos, slopes, lam):
    s = jnp.einsum('bqhcd,bkhcd->bhcqk', q.astype(jnp.float32), k.astype(jnp.float32)) * (HEAD_DIM ** -0.5)
    dist = q_pos[:, None] - k_pos[None, :]
    bias = -slopes[:, None, None] * dist.astype(jnp.float32)
    s = jnp.where((dist >= 0)[None, None, None], s + bias[None, :, None], -jnp.inf)
    p = jax.nn.softmax(s, axis=-1)
    a = p[:, :, 0] - lam * p[:, :, 1]
    o = jnp.einsum('bhqk,bkhe->bqhe', a, v.astype(jnp.float32))
    return o.astype(v.dtype)


def _diff_attn(q, k, v, q_pos, k_pos, slopes, lam):
    B, T = q.shape[0], q.shape[1]
    if T <= Q_BLOCK:
        return _diff_attn_block(q, k, v, q_pos, k_pos, slopes, lam)
    nb = -(-T // Q_BLOCK)
    pad = nb * Q_BLOCK - T
    qp = jnp.pad(q, ((0, 0), (0, pad), (0, 0), (0, 0), (0, 0)))
    pp = jnp.pad(q_pos, (0, pad), mode='edge')
    qb = jnp.moveaxis(qp.reshape(B, nb, Q_BLOCK, N_HEADS, 2, HEAD_DIM), 1, 0)
    pb = pp.reshape(nb, Q_BLOCK)
    ob = lax.map(lambda a: _diff_attn_block(a[0], k, v, a[1], k_pos, slopes, lam), (qb, pb))
    return jnp.moveaxis(ob, 0, 1).reshape(B, nb * Q_BLOCK, N_HEADS, V_DIM)[:, :T]


def _hier_moe(x, w_grp, b_grp, w_rt, b_rt, w_gate, w_up, w_down):
    B, T, _ = x.shape
    lg = (x @ w_grp).astype(jnp.float32) + b_grp.astype(jnp.float32)
    g_idx = jnp.argmax(lg, axis=-1)
    p_grp = jnp.take_along_axis(jax.nn.softmax(lg, axis=-1), g_idx[..., None], axis=-1)
    le = ((x @ w_rt).astype(jnp.float32) + b_rt.astype(jnp.float32)).reshape(B, T, N_GROUPS, EXPERTS_PER_GROUP)
    le = jnp.take_along_axis(le, g_idx[..., None, None], axis=2)[:, :, 0]
    top_v, top_i = lax.top_k(le, TOP_K)
    w = p_grp * jax.nn.softmax(top_v, axis=-1)
    e_idx = g_idx[..., None] * EXPERTS_PER_GROUP + top_i
    comb = jnp.sum(jax.nn.one_hot(e_idx, N_EXPERTS, dtype=jnp.float32) * w[..., None], axis=-2)
    hg = jnp.einsum('btd,edf->btef', x, w_gate)
    hu = jnp.einsum('btd,edf->btef', x, w_up)
    h = jax.nn.silu(hg) * hu * comb[..., None].astype(x.dtype)
    return jnp.einsum('btef,efd->btd', h, w_down)


def _layer(x, past_k, past_v, conv_state, lam_init, norm_mix, w_in, conv_w, w_out_conv, q_norm, k_norm,
           lambda_q1, lambda_k1, lambda_q2, lambda_k2, subln, w_out_attn, w_out, norm_ffn,
           w_grp, b_grp, w_rt, b_rt, w_gate, w_up, w_down):
    B, T, _ = x.shape
    start = 0 if past_k is None else past_k.shape[1]
    xn = _rms(x, norm_mix)
    z = xn @ w_in
    b_g, c_g, xc, q, k, v, gz = jnp.split(z, np.cumsum(IN_SPLITS)[:-1].tolist(), axis=-1)
    u = c_g * xc
    up = jnp.concatenate([conv_state.astype(u.dtype), u], axis=1)
    conv = sum(conv_w[j] * up[:, j:j + T] for j in range(CONV_K))
    y_a = (b_g * conv) @ w_out_conv
    new_conv = up[:, -(CONV_K - 1):]
    q = _rms(q.reshape(B, T, N_HEADS, 2, HEAD_DIM), q_norm)
    k = _rms(k.reshape(B, T, N_HEADS, 2, HEAD_DIM), k_norm)
    v = v.reshape(B, T, N_HEADS, V_DIM)
    if past_k is None:
        k_all, v_all = k, v
    else:
        k_all = jnp.concatenate([past_k.reshape(B, start, N_HEADS, 2, HEAD_DIM).astype(k.dtype), k], axis=1)
        v_all = jnp.concatenate([past_v.astype(v.dtype), v], axis=1)
    lam = (jnp.exp(jnp.sum(lambda_q1.astype(jnp.float32) * lambda_k1.astype(jnp.float32)))
           - jnp.exp(jnp.sum(lambda_q2.astype(jnp.float32) * lambda_k2.astype(jnp.float32))) + lam_init)
    q_pos = start + jnp.arange(T, dtype=jnp.int32)
    k_pos = jnp.arange(start + T, dtype=jnp.int32)
    o = _diff_attn(q, k_all, v_all, q_pos, k_pos, _alibi_slopes(), lam)
    o = _rms(o, subln) * (1.0 - lam_init)
    y_b = o.reshape(B, T, ATTN_W) @ w_out_attn
    g = jax.nn.sigmoid(gz.astype(jnp.float32)).astype(x.dtype).reshape(B, T, 2, D_MODEL)
    h = x + (g[:, :, 0] * y_a + g[:, :, 1] * y_b) @ w_out
    out = h + _hier_moe(_rms(h, norm_ffn), w_grp, b_grp, w_rt, b_rt, w_gate, w_up, w_down)
    return out, k.reshape(B, T, N_HEADS, 2 * HEAD_DIM), v, new_conv


def setup_inputs(seed: int = 0) -> dict:
    key = jax.random.key(seed)
    ks = jax.random.split(key, 32)
    f32 = jnp.float32
    n_pages = PAST_LEN // PAGE_SIZE
    n_used = DEC_BATCH * n_pages
    n_pool = n_used + max(1, n_used // 4)
    nrm = lambda k, shape, s: jax.random.normal(k, shape, f32) * s
    page_table = jax.random.permutation(ks[5], n_pool)[:n_used].reshape(DEC_BATCH, n_pages).astype(jnp.int32)
    return {
        'x_prompt': nrm(ks[0], (BATCH, SEQ, D_MODEL), 1.0),
        'x_sample': nrm(ks[1], (DEC_BATCH, DEC_SEQ, D_MODEL), 1.0),
        'cache_k': nrm(ks[2], (DEPTH, n_pool, PAGE_SIZE, N_HEADS, 2 * HEAD_DIM), 1.0),
        'cache_v': nrm(ks[3], (DEPTH, n_pool, PAGE_SIZE, N_HEADS, 2 * HEAD_DIM), 1.0),
        'state_conv': nrm(ks[4], (DEPTH, DEC_BATCH, CONV_K - 1, W_CONV), 1.0),
        'page_table': page_table,
        'norm_mix': 1.0 + nrm(ks[6], (DEPTH, D_MODEL), 0.02),
        'w_in': nrm(ks[7], (DEPTH, D_MODEL, IN_COLS), D_MODEL ** -0.5),
        'conv_w': nrm(ks[8], (DEPTH, CONV_K, W_CONV), CONV_K ** -0.5),
        'w_out_conv': nrm(ks[9], (DEPTH, W_CONV, D_MODEL), W_CONV ** -0.5),
        'q_norm': 1.0 + nrm(ks[10], (DEPTH, HEAD_DIM), 0.02),
        'k_norm': 1.0 + nrm(ks[11], (DEPTH, HEAD_DIM), 0.02),
        'lambda_q1': nrm(ks[12], (DEPTH, HEAD_DIM), 0.1),
        'lambda_k1': nrm(ks[13], (DEPTH, HEAD_DIM), 0.1),
        'lambda_q2': nrm(ks[14], (DEPTH, HEAD_DIM), 0.1),
        'lambda_k2': nrm(ks[15], (DEPTH, HEAD_DIM), 0.1),
        'subln': 1.0 + nrm(ks[16], (DEPTH, V_DIM), 0.02),
        'w_out_attn': nrm(ks[17], (DEPTH, ATTN_W, D_MODEL), ATTN_W ** -0.5),
        'w_out': nrm(ks[18], (DEPTH, D_MODEL, D_MODEL), D_MODEL ** -0.5),
        'norm_ffn': 1.0 + nrm(ks[19], (DEPTH, D_MODEL), 0.02),
        'w_grp': nrm(ks[20], (DEPTH, D_MODEL, N_GROUPS), D_MODEL ** -0.5),
        'b_grp': nrm(ks[21], (DEPTH, N_GROUPS), 0.01),
        'w_rt': nrm(ks[22], (DEPTH, D_MODEL, N_EXPERTS), D_MODEL ** -0.5),
        'b_rt': nrm(ks[23], (DEPTH, N_EXPERTS), 0.01),
        'w_gate': nrm(ks[24], (DEPTH, N_EXPERTS, D_MODEL, D_EXPERT), D_MODEL ** -0.5),
        'w_up': nrm(ks[25], (DEPTH, N_EXPERTS, D_MODEL, D_EXPERT), D_MODEL ** -0.5),
        'w_down': nrm(ks[26], (DEPTH, N_EXPERTS, D_EXPERT, D_MODEL), D_EXPERT ** -0.5),
    }


def reference(x_prompt, x_sample, cache_k, cache_v, state_conv, page_table, norm_mix, w_in, conv_w,
              w_out_conv, q_norm, k_norm, lambda_q1, lambda_k1, lambda_q2, lambda_k2, subln, w_out_attn,
              w_out, norm_ffn, w_grp, b_grp, w_rt, b_rt, w_gate, w_up, w_down):
    n_dec = x_sample.shape[0]
    past_len = page_table.shape[1] * cache_k.shape[2]
    xp, xs = x_prompt, x_sample
    kp_l, vp_l, cp_l, ks_l, vs_l, cs_l = [], [], [], [], [], []
    for l in range(DEPTH):
        lam_init = 0.8 - 0.6 * math.exp(-0.3 * l)
        lp = [a[l] for a in (norm_mix, w_in, conv_w, w_out_conv, q_norm, k_norm, lambda_q1, lambda_k1,
                             lambda_q2, lambda_k2, subln, w_out_attn, w_out, norm_ffn, w_grp, b_grp,
                             w_rt, b_rt, w_gate, w_up, w_down)]
        zero_conv = jnp.zeros((xp.shape[0], CONV_K - 1, W_CONV), xp.dtype)
        xp, kp, vp, cp = _layer(xp, None, None, zero_conv, lam_init, *lp)
        past_k = cache_k[l][page_table].reshape(n_dec, past_len, N_HEADS, 2 * HEAD_DIM)
        past_v = cache_v[l][page_table].reshape(n_dec, past_len, N_HEADS, V_DIM)
        xs, ksn, vsn, csn = _layer(xs, past_k, past_v, state_conv[l], lam_init, *lp)
        kp_l.append(kp); vp_l.append(vp); cp_l.append(cp)
        ks_l.append(ksn); vs_l.append(vsn); cs_l.append(csn)
    k_prompt = jnp.stack(kp_l)
    v_prompt = jnp.stack(vp_l)
    conv_prompt = jnp.stack(cp_l)
    k_sample = jnp.stack(ks_l)
    v_sample = jnp.stack(vs_l)
    conv_sample = jnp.stack(cs_l)
    return (xp, xs, k_prompt, v_prompt, conv_prompt, k_sample, v_sample, conv_sample)
```

```python
import functools
import math

import numpy as np
import jax
import jax.numpy as jnp
from jax import lax
from jax.experimental import pallas as pl
from jax.experimental.pallas import tpu as pltpu

f32 = jnp.float32
bf16 = jnp.bfloat16
i32 = jnp.int32

D_MODEL = 2048
W_CONV = 1024
CONV_K = 3
N_HEADS = 8
HEAD_DIM = 64
V_DIM = 128
ATTN_W = 1024
N_GROUPS = 4
EXPERTS_PER_GROUP = 8
N_EXPERTS = 32
D_EXPERT = 512
NORM_EPS = 1e-6
LOG2E = 1.4426950408889634

LANES = 128
SUBLANES = 8
SLAB = D_MODEL // LANES
VMEM_LIMIT = 56 * 1024 * 1024
NEG_BIG = -1e30

HI = lax.Precision.HIGHEST


def _cparams(n_axes, vmem=VMEM_LIMIT):
    return pltpu.CompilerParams(
        dimension_semantics=("arbitrary",) * n_axes, vmem_limit_bytes=vmem)


def _const_spec(shape):
    nd = len(shape)
    return pl.BlockSpec(shape, lambda *_: (0,) * nd, pipeline_mode=pl.Buffered(1))


def _smem_spec():
    return pl.BlockSpec(memory_space=pltpu.SMEM)


def _rms_kernel(x_ref, g_ref, o_ref):
    x = x_ref[...]
    ms = jnp.mean(x * x, axis=-1, keepdims=True)
    o_ref[...] = (x * lax.rsqrt(ms + NORM_EPS) * g_ref[...]).astype(o_ref.dtype)


def _rmsnorm(x, g, out_dtype, tm):
    t, d = x.shape
    return pl.pallas_call(
        _rms_kernel,
        out_shape=jax.ShapeDtypeStruct((t, d), out_dtype),
        grid=(t // tm,),
        in_specs=[pl.BlockSpec((tm, d), lambda i: (i, 0)), _const_spec((1, d))],
        out_specs=pl.BlockSpec((tm, d), lambda i: (i, 0)),
        compiler_params=_cparams(1),
        name="rmsnorm",
    )(x, g.reshape(1, d))


def _conv_kernel(xn_ref, w_ref, cw_ref, a_ref, tail_ref, ubuf, *, tm):
    i = pl.program_id(0)

    @pl.when(i == 0)
    def _():
        ubuf[0:SUBLANES, :] = jnp.zeros((SUBLANES, W_CONV), f32)

    z = jnp.dot(xn_ref[...], w_ref[...], preferred_element_type=f32)
    bg = z[:, :W_CONV]
    u = z[:, W_CONV:2 * W_CONV] * z[:, 2 * W_CONV:]
    ubuf[SUBLANES:SUBLANES + tm, :] = u
    cw = cw_ref[...]
    conv = (cw[0:1] * ubuf[SUBLANES - 2:SUBLANES - 2 + tm, :]
            + cw[1:2] * ubuf[SUBLANES - 1:SUBLANES - 1 + tm, :]
            + cw[2:3] * u)
    a_ref[...] = (bg * conv).astype(a_ref.dtype)
    tail = ubuf[tm:tm + SUBLANES, :]
    ubuf[0:SUBLANES, :] = tail
    tail_ref[...] = tail


def _conv_branch(xn, w3, conv_w, tm):
    t = xn.shape[0]
    return pl.pallas_call(
        functools.partial(_conv_kernel, tm=tm),
        out_shape=(jax.ShapeDtypeStruct((t, W_CONV), bf16),
                   jax.ShapeDtypeStruct((SUBLANES, W_CONV), f32)),
        grid=(t // tm,),
        in_specs=[pl.BlockSpec((tm, D_MODEL), lambda i: (i, 0)),
                  _const_spec((D_MODEL, 3 * W_CONV)),
                  _const_spec((CONV_K, W_CONV))],
        out_specs=(pl.BlockSpec((tm, W_CONV), lambda i: (i, 0)),
                   pl.BlockSpec((SUBLANES, W_CONV), lambda i: (0, 0))),
        scratch_shapes=[pltpu.VMEM((tm + SUBLANES, W_CONV), f32)],
        compiler_params=_cparams(1),
        name="conv_branch",
    )(xn, w3, conv_w)


def _norm_halves(z, g):
    z2 = z * z
    lo = lax.broadcasted_iota(i32, z.shape, z.ndim - 1) < HEAD_DIM
    s_lo = jnp.sum(jnp.where(lo, z2, 0.0), axis=-1, keepdims=True)
    s_hi = jnp.sum(jnp.where(lo, 0.0, z2), axis=-1, keepdims=True)
    ms = jnp.where(lo, s_lo, s_hi) * (1.0 / HEAD_DIM)
    return z * lax.rsqrt(ms + NORM_EPS) * g, lo


def _qkv_kernel(xn_ref, w_ref, qg_ref, kg_ref, qa_ref, qb_ref, k32_ref, kb_ref, v32_ref, vb_ref):
    z = jnp.dot(xn_ref[...], w_ref[...], preferred_element_type=f32)
    qg = qg_ref[...]
    kg = kg_ref[...]
    for h in range(N_HEADS):
        sl = slice(h * LANES, (h + 1) * LANES)
        qn, lo = _norm_halves(z[:, sl], qg)
        qa_ref[:, sl] = jnp.where(lo, qn, 0.0).astype(bf16)
        qb_ref[:, sl] = jnp.where(lo, 0.0, qn).astype(bf16)
        kn, _ = _norm_halves(z[:, ATTN_W + h * LANES:ATTN_W + (h + 1) * LANES], kg)
        k32_ref[:, sl] = kn
        kb_ref[:, sl] = kn.astype(bf16)
    v = z[:, 2 * ATTN_W:]
    v32_ref[...] = v
    vb_ref[...] = v.astype(bf16)


def _qkv(xn, w_qkv, qg, kg, tm):
    t = xn.shape[0]
    row = lambda dt: jax.ShapeDtypeStruct((t, ATTN_W), dt)
    blk = pl.BlockSpec((tm, ATTN_W), lambda i: (i, 0))
    return pl.pallas_call(
        _qkv_kernel,
        out_shape=(row(bf16), row(bf16), row(f32), row(bf16), row(f32), row(bf16)),
        grid=(t // tm,),
        in_specs=[pl.BlockSpec((tm, D_MODEL), lambda i: (i, 0)),
                  _const_spec((D_MODEL, 3 * ATTN_W)),
                  _const_spec((1, LANES)), _const_spec((1, LANES))],
        out_specs=(blk,) * 6,
        compiler_params=_cparams(1),
        name="qkv_proj",
    )(xn, w_qkv, qg, kg)


def _attn_kernel(par_ref, qa_ref, qb_ref, k_ref, v_ref, sub_ref, o_ref,
                 acc1, acc2, m1, m2, bias, *, tq, tk):
    h = pl.program_id(0)
    qi = pl.program_id(1)
    slope = par_ref[h]
    lam = par_ref[N_HEADS]
    out_scale = par_ref[N_HEADS + 1]

    @pl.when(qi == 0)
    def _():
        row = lax.broadcasted_iota(i32, (tq, tk), 0)
        col = lax.broadcasted_iota(i32, (tq, tk), 1)
        b = (col - row).astype(f32) * slope
        bias[0] = b
        bias[1] = jnp.where(col <= row, b, NEG_BIG)

    acc1[...] = jnp.zeros_like(acc1)
    acc2[...] = jnp.zeros_like(acc2)
    m1[...] = jnp.full_like(m1, NEG_BIG)
    m2[...] = jnp.full_like(m2, NEG_BIG)
    qa = qa_ref[...]
    qb = qb_ref[...]
    ones = jnp.ones((tk, LANES), bf16)

    def tile(kv, diag):
        k0 = pl.multiple_of(kv * tk, tk)
        k = k_ref[pl.ds(k0, tk), :]
        vext = jnp.concatenate([v_ref[pl.ds(k0, tk), :], ones], axis=1)
        c_off = (k0 - qi * tq).astype(f32) * slope
        b = bias[diag]
        for q, acc, m in ((qa, acc1, m1), (qb, acc2, m2)):
            s = lax.dot_general(q, k, (((1,), (1,)), ((), ())),
                                preferred_element_type=f32) + b
            m_prev = m[...]
            m_new = jnp.maximum(m_prev, jnp.max(s, axis=-1, keepdims=True) + c_off)
            p = jnp.exp2(s - (m_new - c_off))
            alpha = jnp.exp2(m_prev - m_new)
            acc[...] = alpha * acc[...] + jnp.dot(p.astype(bf16), vext,
                                                  preferred_element_type=f32)
            m[...] = m_new

    def body(kv, carry):
        tile(kv, 0)
        return carry

    lax.fori_loop(0, qi, body, 0)
    tile(qi, 1)

    a1 = acc1[...]
    a2 = acc2[...]
    o = a1[:, :V_DIM] / a1[:, V_DIM:] - lam * (a2[:, :V_DIM] / a2[:, V_DIM:])
    ms = jnp.mean(o * o, axis=-1, keepdims=True)
    o_ref[...] = (o * lax.rsqrt(ms + NORM_EPS) * sub_ref[...] * out_scale).astype(o_ref.dtype)


def _attention(par, qa, qb, kb, vb, sub, tq):
    t = qa.shape[0]
    tk = tq
    qspec = pl.BlockSpec((tq, LANES), lambda h, i: (i, h))
    kspec = pl.BlockSpec((t, LANES), lambda h, i: (0, h))
    return pl.pallas_call(
        functools.partial(_attn_kernel, tq=tq, tk=tk),
        out_shape=jax.ShapeDtypeStruct((t, ATTN_W), bf16),
        grid=(N_HEADS, t // tq),
        in_specs=[_smem_spec(), qspec, qspec, kspec, kspec, _const_spec((1, V_DIM))],
        out_specs=pl.BlockSpec((tq, LANES), lambda h, i: (i, h)),
        scratch_shapes=[pltpu.VMEM((tq, 2 * V_DIM), f32), pltpu.VMEM((tq, 2 * V_DIM), f32),
                        pltpu.VMEM((tq, 1), f32), pltpu.VMEM((tq, 1), f32),
                        pltpu.VMEM((2, tq, tk), f32)],
        compiler_params=_cparams(2),
        name="diff_attention",
    )(par, qa, qb, kb, vb, sub)


def _gate_kernel(xn_ref, w_ref, g_ref):
    z = jnp.dot(xn_ref[...], w_ref[...], preferred_element_type=f32)
    g_ref[...] = jax.nn.sigmoid(z).astype(g_ref.dtype)


def _gates(xn, w_g, tm, tn):
    t = xn.shape[0]
    n = w_g.shape[1]
    return pl.pallas_call(
        _gate_kernel,
        out_shape=jax.ShapeDtypeStruct((t, n), bf16),
        grid=(n // tn, t // tm),
        in_specs=[pl.BlockSpec((tm, D_MODEL), lambda j, i: (i, 0)),
                  pl.BlockSpec((D_MODEL, tn), lambda j, i: (0, j))],
        out_specs=pl.BlockSpec((tm, tn), lambda j, i: (i, j)),
        compiler_params=_cparams(2),
        name="merge_gates",
    )(xn, w_g)


def _mix_kernel(a_ref, o_ref, g_ref, x_ref, woc_ref, woa_ref, wo_ref, h_ref):
    ya = jnp.dot(a_ref[...], woc_ref[...], preferred_element_type=f32)
    yb = jnp.dot(o_ref[...], woa_ref[...], preferred_element_type=f32)
    g = g_ref[...].astype(f32)
    m = g[:, :D_MODEL] * ya + g[:, D_MODEL:] * yb
    h_ref[...] = x_ref[...] + jnp.dot(m.astype(bf16), wo_ref[...], preferred_element_type=f32)


def _mix(a, o, g, x, w_oc, w_oa, w_o, tm):
    t = x.shape[0]
    rows = lambda n: pl.BlockSpec((tm, n), lambda i: (i, 0))
    return pl.pallas_call(
        _mix_kernel,
        out_shape=jax.ShapeDtypeStruct((t, D_MODEL), f32),
        grid=(t // tm,),
        in_specs=[rows(W_CONV), rows(ATTN_W), rows(2 * D_MODEL), rows(D_MODEL),
                  _const_spec((W_CONV, D_MODEL)), _const_spec((ATTN_W, D_MODEL)),
                  _const_spec((D_MODEL, D_MODEL))],
        out_specs=rows(D_MODEL),
        compiler_params=_cparams(1),
        name="mix_out_proj",
    )(a, o, g, x, w_oc, w_oa, w_o)


def _lane_first(mask, lane_f):
    return jnp.min(jnp.where(mask, lane_f, float(LANES)), axis=-1, keepdims=True)


def _router_kernel(h_ref, g_ref, w_ref, b_ref, c0_ref, xs_ref, idx_ref, wt_ref, cnt_ref,
                   run, *, tm, precise):
    i = pl.program_id(0)

    @pl.when(i == 0)
    def _():
        run[...] = c0_ref[...]

    h = h_ref[...]
    ms = jnp.mean(h * h, axis=-1, keepdims=True)
    xn = h * lax.rsqrt(ms + NORM_EPS) * g_ref[...]
    for c in range(SLAB):
        xs_ref[pl.ds(c, tm, stride=SLAB), :] = xn[:, c * LANES:(c + 1) * LANES]

    if precise:
        lg = jnp.dot(xn, w_ref[...], preferred_element_type=f32, precision=HI)
    else:
        lg = jnp.dot(xn.astype(bf16), w_ref[...].astype(bf16), preferred_element_type=f32)
    lg = lg + b_ref[...]

    lane = lax.broadcasted_iota(i32, (tm, LANES), 1)
    lane_f = lane.astype(f32)
    gm = lane < N_GROUPS
    lgm = jnp.where(gm, lg, -jnp.inf)
    gmax = jnp.max(lgm, axis=-1, keepdims=True)
    g_lane = _lane_first(lgm == gmax, lane_f)
    p_grp = 1.0 / jnp.sum(jnp.exp(lgm - gmax), axis=-1, keepdims=True)

    e_lo = N_GROUPS + EXPERTS_PER_GROUP * g_lane
    em = (lane_f >= e_lo) & (lane_f < e_lo + EXPERTS_PER_GROUP)
    lem = jnp.where(em, lg, -jnp.inf)
    v1 = jnp.max(lem, axis=-1, keepdims=True)
    l1 = _lane_first(lem == v1, lane_f)
    lem2 = jnp.where(lane_f == l1, -jnp.inf, lem)
    v2 = jnp.max(lem2, axis=-1, keepdims=True)
    l2 = _lane_first(lem2 == v2, lane_f)
    e21 = jnp.exp(v2 - v1)
    w1 = p_grp / (1.0 + e21)
    w2 = w1 * e21

    oh1 = lane_f == l1
    oh2 = lane_f == l2
    oh = jnp.where(oh1 | oh2, 1.0, 0.0)
    r_i = lax.broadcasted_iota(i32, (tm, tm), 0)
    c_i = lax.broadcasted_iota(i32, (tm, tm), 1)
    tril = jnp.where(c_i < r_i, 1.0, 0.0).astype(bf16)
    before = jnp.dot(tril, oh.astype(bf16), preferred_element_type=f32) + run[0:1, :]
    r1 = jnp.sum(jnp.where(oh1, before, 0.0), axis=-1, keepdims=True)
    r2 = jnp.sum(jnp.where(oh2, before, 0.0), axis=-1, keepdims=True)
    run[...] = run[...] + jnp.sum(oh, axis=0, keepdims=True)

    out = jnp.where(lane == 0, l1 - N_GROUPS,
                    jnp.where(lane == 1, l2 - N_GROUPS,
                              jnp.where(lane == 2, r1, jnp.where(lane == 3, r2, 0.0))))
    idx_ref[...] = out.astype(i32)
    wt_ref[...] = jnp.where(lane == 0, w1, jnp.where(lane == 1, w2, 0.0))
    cnt_ref[...] = run[...]


def _router(h, g, w_r, b_r, counts0, tm, precise):
    t = h.shape[0]
    rows = lambda n, r=1: pl.BlockSpec((tm * r, n), lambda i: (i, 0))
    return pl.pallas_call(
        functools.partial(_router_kernel, tm=tm, precise=precise),
        out_shape=(jax.ShapeDtypeStruct((t * SLAB, LANES), f32),
                   jax.ShapeDtypeStruct((t, LANES), i32),
                   jax.ShapeDtypeStruct((t, LANES), f32),
                   jax.ShapeDtypeStruct((SUBLANES, LANES), f32)),
        grid=(t // tm,),
        in_specs=[rows(D_MODEL), _const_spec((1, D_MODEL)), _const_spec((D_MODEL, LANES)),
                  _const_spec((1, LANES)), _const_spec((SUBLANES, LANES))],
        out_specs=(rows(LANES, SLAB), rows(LANES), rows(LANES),
                   pl.BlockSpec((SUBLANES, LANES), lambda i: (0, 0))),
        scratch_shapes=[pltpu.VMEM((SUBLANES, LANES), f32)],
        compiler_params=_cparams(1),
        name="router",
    )(h, g.reshape(1, D_MODEL), w_r, b_r, counts0)


def _dispatch_kernel(pos_p_ref, pos_s_ref, xp_ref, xs_ref, out_ref, sem, *, chunk, n_s):
    i = pl.program_id(0)

    def copy(src_ref, tok, pos):
        return pltpu.make_async_copy(
            src_ref.at[pl.ds(pl.multiple_of(tok * SLAB, SLAB), SLAB), :],
            out_ref.at[pl.ds(pl.multiple_of(pos * SLAB, SLAB), SLAB), :], sem)

    def scatter(src_ref, pos_ref, tok0, n):
        def issue(j, carry):
            copy(src_ref, tok0 + j // 2, pos_ref[2 * tok0 + j]).start()
            return carry

        def drain(j, carry):
            copy(src_ref, 0, 0).wait()
            return carry

        lax.fori_loop(0, 2 * n, issue, 0)
        lax.fori_loop(0, 2 * n, drain, 0)

    scatter(xp_ref, pos_p_ref, i * chunk, chunk)

    @pl.when(i == 0)
    def _():
        scatter(xs_ref, pos_s_ref, 0, n_s)


def _dispatch(pos_p, pos_s, xp, xs, n_rows, chunk):
    n_p = xp.shape[0] // SLAB
    n_s = xs.shape[0] // SLAB
    any_spec = pl.BlockSpec(memory_space=pl.ANY)
    return pl.pallas_call(
        functools.partial(_dispatch_kernel, chunk=chunk, n_s=n_s),
        out_shape=jax.ShapeDtypeStruct((n_rows * SLAB, LANES), f32),
        grid_spec=pltpu.PrefetchScalarGridSpec(
            num_scalar_prefetch=2, grid=(n_p // chunk,),
            in_specs=[any_spec, any_spec], out_specs=any_spec,
            scratch_shapes=[pltpu.SemaphoreType.DMA]),
        compiler_params=pltpu.CompilerParams(dimension_semantics=("arbitrary",),
                                             has_side_effects=True),
        name="moe_dispatch",
    )(pos_p, pos_s, xp, xs)


def _moe_kernel(sched_ref, x_ref, wg_ref, wu_ref, wd_ref, y_ref, *, tg, n_tiles):
    t = pl.program_id(0)

    @pl.when(t < sched_ref[2 * n_tiles])
    def _():
        valid = sched_ref[n_tiles + t]
        x = jnp.concatenate([x_ref[pl.ds(c, tg, stride=SLAB), :] for c in range(SLAB)], axis=1)
        row = lax.broadcasted_iota(i32, (tg, 1), 0)
        xb = jnp.where(row < valid, x, 0.0).astype(bf16)
        hg = jnp.dot(xb, wg_ref[...].astype(bf16), preferred_element_type=f32)
        hu = jnp.dot(xb, wu_ref[...].astype(bf16), preferred_element_type=f32)
        act = (hg * jax.nn.sigmoid(hg) * hu).astype(bf16)
        y = jnp.dot(act, wd_ref[...].astype(bf16), preferred_element_type=f32)
        for c in range(SLAB):
            y_ref[pl.ds(c, tg, stride=SLAB), :] = y[:, c * LANES:(c + 1) * LANES]


def _moe(sched, xs, w_gate, w_up, w_down, tg, n_tiles):
    def row_map(t, s):
        return (jnp.minimum(t, s[2 * n_tiles] - 1), 0)

    def w_map(t, s):
        return (s[t], 0, 0)

    return pl.pallas_call(
        functools.partial(_moe_kernel, tg=tg, n_tiles=n_tiles),
        out_shape=jax.ShapeDtypeStruct((n_tiles * tg * SLAB, LANES), f32),
        grid_spec=pltpu.PrefetchScalarGridSpec(
            num_scalar_prefetch=1, grid=(n_tiles,),
            in_specs=[pl.BlockSpec((tg * SLAB, LANES), row_map),
                      pl.BlockSpec((None, D_MODEL, D_EXPERT), w_map),
                      pl.BlockSpec((None, D_MODEL, D_EXPERT), w_map),
                      pl.BlockSpec((None, D_EXPERT, D_MODEL), w_map)],
            out_specs=pl.BlockSpec((tg * SLAB, LANES), row_map)),
        compiler_params=_cparams(1),
        name="moe_grouped_swiglu",
    )(sched, xs, w_gate, w_up, w_down)


def _combine_kernel(pos_ref, h_ref, wt_ref, y_ref, o_ref, buf, sem, *, tm):
    i = pl.program_id(0)

    def copy(j, pos):
        return pltpu.make_async_copy(
            y_ref.at[pl.ds(pl.multiple_of(pos * SLAB, SLAB), SLAB), :],
            buf.at[pl.ds(pl.multiple_of(j * SLAB, SLAB), SLAB), :], sem)

    def issue(j, carry):
        r = j // 2
        k = j - 2 * r
        copy(k * tm + r, pos_ref[2 * i * tm + j]).start()
        return carry

    def drain(j, carry):
        copy(0, 0).wait()
        return carry

    lax.fori_loop(0, 2 * tm, issue, 0)
    lax.fori_loop(0, 2 * tm, drain, 0)

    wt = wt_ref[...]
    w0 = wt[:, 0:1]
    w1 = wt[:, 1:2]
    for c in range(SLAB):
        sl = slice(c * LANES, (c + 1) * LANES)
        y0 = buf[pl.ds(c, tm, stride=SLAB), :]
        y1 = buf[pl.ds(tm * SLAB + c, tm, stride=SLAB), :]
        o_ref[:, sl] = h_ref[:, sl] + w0 * y0 + w1 * y1


def _combine(pos, h, wt, y, tm):
    t = h.shape[0]
    return pl.pallas_call(
        functools.partial(_combine_kernel, tm=tm),
        out_shape=jax.ShapeDtypeStruct((t, D_MODEL), f32),
        grid_spec=pltpu.PrefetchScalarGridSpec(
            num_scalar_prefetch=1, grid=(t // tm,),
            in_specs=[pl.BlockSpec((tm, D_MODEL), lambda i, p: (i, 0)),
                      pl.BlockSpec((tm, LANES), lambda i, p: (i, 0)),
                      pl.BlockSpec(memory_space=pl.ANY)],
            out_specs=pl.BlockSpec((tm, D_MODEL), lambda i, p: (i, 0)),
            scratch_shapes=[pltpu.VMEM((2 * tm * SLAB, LANES), f32),
                            pltpu.SemaphoreType.DMA]),
        compiler_params=_cparams(1),
        name="moe_combine",
    )(pos, h, wt, y)


def _mm_kernel(x_ref, w_ref, o_ref):
    o_ref[...] = jnp.dot(x_ref[...], w_ref[...], preferred_element_type=f32, precision=HI)


def _mm_small(x, w, tn):
    m, k = x.shape
    n = w.shape[1]
    return pl.pallas_call(
        _mm_kernel,
        out_shape=jax.ShapeDtypeStruct((m, n), f32),
        grid=(n // tn,),
        in_specs=[_const_spec((m, k)), pl.BlockSpec((k, tn), lambda j: (0, j))],
        out_specs=pl.BlockSpec((m, tn), lambda j: (0, j)),
        compiler_params=_cparams(1),
        name="decode_matmul",
    )(x, w)


def _mm_gate_kernel(a_ref, o_ref, g0_ref, g1_ref, woc_ref, woa_ref, m_ref):
    ya = jnp.dot(a_ref[...], woc_ref[...], preferred_element_type=f32, precision=HI)
    yb = jnp.dot(o_ref[...], woa_ref[...], preferred_element_type=f32, precision=HI)
    m_ref[...] = g0_ref[...] * ya + g1_ref[...] * yb


def _mm_gate_small(a, o, g, w_oc, w_oa, tn):
    m = a.shape[0]
    nb = D_MODEL // tn
    return pl.pallas_call(
        _mm_gate_kernel,
        out_shape=jax.ShapeDtypeStruct((m, D_MODEL), f32),
        grid=(nb,),
        in_specs=[_const_spec((m, W_CONV)), _const_spec((m, ATTN_W)),
                  pl.BlockSpec((m, tn), lambda j: (0, j)),
                  pl.BlockSpec((m, tn), lambda j: (0, nb + j)),
                  pl.BlockSpec((W_CONV, tn), lambda j: (0, j)),
                  pl.BlockSpec((ATTN_W, tn), lambda j: (0, j))],
        out_specs=pl.BlockSpec((m, tn), lambda j: (0, j)),
        compiler_params=_cparams(1),
        name="decode_gated_proj",
    )(a, o, g, g, w_oc, w_oa)


def _mm_res_kernel(x_ref, m_ref, w_ref, h_ref):
    h_ref[...] = x_ref[...] + jnp.dot(m_ref[...], w_ref[...], preferred_element_type=f32,
                                      precision=HI)


def _mm_res_small(x, mm, w, tn):
    m = x.shape[0]
    return pl.pallas_call(
        _mm_res_kernel,
        out_shape=jax.ShapeDtypeStruct((m, D_MODEL), f32),
        grid=(D_MODEL // tn,),
        in_specs=[pl.BlockSpec((m, tn), lambda j: (0, j)), _const_spec((m, D_MODEL)),
                  pl.BlockSpec((D_MODEL, tn), lambda j: (0, j))],
        out_specs=pl.BlockSpec((m, tn), lambda j: (0, j)),
        compiler_params=_cparams(1),
        name="decode_out_proj",
    )(x, mm, w)


def _dec_pre_kernel(z_ref, s0_ref, s1_ref, cw_ref, qg_ref, kg_ref,
                    a_ref, u_ref, q_ref, k_ref, g_ref):
    bg = z_ref[:, 0:W_CONV]
    u = z_ref[:, W_CONV:2 * W_CONV] * z_ref[:, 2 * W_CONV:3 * W_CONV]
    cw = cw_ref[...]
    conv = cw[0:1] * s0_ref[...] + cw[1:2] * s1_ref[...] + cw[2:3] * u
    a_ref[...] = bg * conv
    u_ref[...] = u
    q0 = 3 * W_CONV
    k0 = q0 + ATTN_W
    for h in range(N_HEADS):
        sl = slice(h * LANES, (h + 1) * LANES)
        q_ref[:, sl] = _norm_halves(z_ref[:, q0 + h * LANES:q0 + (h + 1) * LANES], qg_ref[...])[0]
        k_ref[:, sl] = _norm_halves(z_ref[:, k0 + h * LANES:k0 + (h + 1) * LANES], kg_ref[...])[0]
    g_ref[...] = jax.nn.sigmoid(z_ref[:, 3 * W_CONV + 3 * ATTN_W:])


def _dec_pre(z, s0, s1, conv_w, qg, kg):
    m = z.shape[0]
    full = lambda a: pl.BlockSpec(a.shape, lambda: (0,) * a.ndim)
    outs = (jax.ShapeDtypeStruct((m, W_CONV), f32), jax.ShapeDtypeStruct((m, W_CONV), f32),
            jax.ShapeDtypeStruct((m, ATTN_W), f32), jax.ShapeDtypeStruct((m, ATTN_W), f32),
            jax.ShapeDtypeStruct((m, 2 * D_MODEL), f32))
    args = (z, s0, s1, conv_w, qg, kg)
    return pl.pallas_call(
        _dec_pre_kernel,
        out_shape=outs,
        in_specs=[full(a) for a in args],
        out_specs=tuple(pl.BlockSpec(o.shape, lambda: (0, 0)) for o in outs),
        compiler_params=pltpu.CompilerParams(vmem_limit_bytes=VMEM_LIMIT),
        name="decode_pre",
    )(*args)


def _dec_attn_kernel(pt_ref, par_ref, slope_ref, q_ref, kn_ref, vn_ref, sub_ref, *rest,
                     pages, page_size, past_len):
    k_refs = rest[:pages]
    v_refs = rest[pages:2 * pages]
    o_ref = rest[2 * pages]
    acc1, acc2, m1, m2, l1, l2 = rest[2 * pages + 1:]
    p_step = pl.program_id(1)
    n_steps = pl.num_programs(1)

    @pl.when(p_step == 0)
    def _():
        acc1[...] = jnp.zeros_like(acc1)
        acc2[...] = jnp.zeros_like(acc2)
        l1[...] = jnp.zeros_like(l1)
        l2[...] = jnp.zeros_like(l2)
        m1[...] = jnp.full_like(m1, NEG_BIG)
        m2[...] = jnp.full_like(m2, NEG_BIG)

    q = q_ref[...]
    slope = slope_ref[...]
    lo = lax.broadcasted_iota(i32, (N_HEADS, LANES), 1) < HEAD_DIM
    pos = lax.broadcasted_iota(i32, (page_size, N_HEADS, LANES), 0)

    def update(s, vals, acc, m, l):
        m_prev = m[...]
        m_new = jnp.maximum(m_prev, jnp.max(s, axis=0))
        p = jnp.exp(s - m_new[None])
        alpha = jnp.exp(m_prev - m_new)
        l[...] = alpha * l[...] + jnp.sum(p, axis=0)
        acc[...] = alpha * acc[...] + jnp.sum(p * vals, axis=0)
        m[...] = m_new

    def scores(kq):
        s_lo = jnp.sum(jnp.where(lo, kq, 0.0), axis=-1, keepdims=True)
        s_hi = jnp.sum(jnp.where(lo, 0.0, kq), axis=-1, keepdims=True)
        return (jnp.broadcast_to(s_lo, kq.shape), jnp.broadcast_to(s_hi, kq.shape))

    for j in range(pages):
        start = (p_step * pages + j) * page_size
        dist = (past_len - start - pos).astype(f32)
        bias = -slope[None] * dist
        s_lo, s_hi = scores(k_refs[j][...] * q[None])
        vals = v_refs[j][...]
        update(s_lo + bias, vals, acc1, m1, l1)
        update(s_hi + bias, vals, acc2, m2, l2)

    @pl.when(p_step == n_steps - 1)
    def _():
        s_lo, s_hi = scores((kn_ref[...] * q)[None])
        vals = vn_ref[...][None]
        update(s_lo, vals, acc1, m1, l1)
        update(s_hi, vals, acc2, m2, l2)
        lam = par_ref[N_HEADS]
        out_scale = par_ref[N_HEADS + 1]
        o = acc1[...] / l1[...] - lam * (acc2[...] / l2[...])
        ms = jnp.mean(o * o, axis=-1, keepdims=True)
        o_ref[...] = o * lax.rsqrt(ms + NORM_EPS) * sub_ref[...] * out_scale


def _dec_attention(page_table, par, slope_rep, q, k_new, v_new, sub, cache_k, cache_v, pages):
    nb, n_pages = page_table.shape
    page_size = cache_k.shape[1]
    past_len = n_pages * page_size
    head_blk = pl.BlockSpec((None, N_HEADS, LANES), lambda b, p, pt: (b, 0, 0))

    def page_spec(j):
        return pl.BlockSpec((None, page_size, N_HEADS, LANES),
                            lambda b, p, pt: (pt[b, p * pages + j], 0, 0, 0))

    state = pltpu.VMEM((N_HEADS, LANES), f32)
    return pl.pallas_call(
        functools.partial(_dec_attn_kernel, pages=pages, page_size=page_size, past_len=past_len),
        out_shape=jax.ShapeDtypeStruct((nb, N_HEADS, LANES), f32),
        grid_spec=pltpu.PrefetchScalarGridSpec(
            num_scalar_prefetch=1, grid=(nb, n_pages // pages),
            in_specs=[_smem_spec(),
                      pl.BlockSpec((N_HEADS, LANES), lambda b, p, pt: (0, 0)),
                      head_blk, head_blk, head_blk,
                      pl.BlockSpec((1, LANES), lambda b, p, pt: (0, 0))]
                     + [page_spec(j) for j in range(pages)] * 2,
            out_specs=head_blk,
            scratch_shapes=[state] * 6),
        compiler_params=_cparams(2),
        name="decode_attention",
    )(page_table, par, slope_rep, q, k_new, v_new, sub, *([cache_k] * pages), *([cache_v] * pages))


def kernel(x_prompt, x_sample, cache_k, cache_v, state_conv, page_table, norm_mix, w_in, conv_w,
           w_out_conv, q_norm, k_norm, lambda_q1, lambda_k1, lambda_q2, lambda_k2, subln, w_out_attn,
           w_out, norm_ffn, w_grp, b_grp, w_rt, b_rt, w_gate, w_up, w_down):
    depth = w_in.shape[0]
    assert depth == 1 and x_prompt.shape[0] == 1 and x_sample.shape[1] == 1
    l = 0
    lam_init = 0.8 - 0.6 * math.exp(-0.3 * l)
    t_p = x_prompt.shape[1]
    n_dec = x_sample.shape[0]
    xp = x_prompt.reshape(t_p, D_MODEL)
    xs = x_sample.reshape(n_dec, D_MODEL)

    lam = (jnp.exp(jnp.sum(lambda_q1[l] * lambda_k1[l]))
           - jnp.exp(jnp.sum(lambda_q2[l] * lambda_k2[l])) + lam_init).astype(f32)
    slopes = np.exp2(-8.0 * np.arange(1, N_HEADS + 1, dtype=np.float32) / N_HEADS).astype(np.float32)
    tail = jnp.stack([lam, jnp.asarray(1.0 - lam_init, f32)])
    par_p = jnp.concatenate([jnp.asarray(slopes * LOG2E, f32), tail])
    par_s = jnp.concatenate([jnp.asarray(slopes, f32), tail])
    slope_rep = jnp.asarray(np.repeat(slopes[:, None], LANES, axis=1))
    scale = HEAD_DIM ** -0.5
    qg2 = jnp.tile(q_norm[l], 2).reshape(1, LANES)
    kg2 = jnp.tile(k_norm[l], 2).reshape(1, LANES)
    sub = subln[l].reshape(1, V_DIM)

    w_in_l = w_in[l]
    w_router = jnp.pad(jnp.concatenate([w_grp[l], w_rt[l]], axis=1),
                       ((0, 0), (0, LANES - N_GROUPS - N_EXPERTS)))
    b_router = jnp.pad(jnp.concatenate([b_grp[l], b_rt[l]]),
                       (0, LANES - N_GROUPS - N_EXPERTS)).reshape(1, LANES)

    xn_p = _rmsnorm(xp, norm_mix[l], bf16, 512)
    w_conv3 = w_in_l[:, :3 * W_CONV].astype(bf16)
    w_qkv = w_in_l[:, 3 * W_CONV:3 * W_CONV + 3 * ATTN_W].astype(bf16)
    w_g = w_in_l[:, 3 * W_CONV + 3 * ATTN_W:].astype(bf16)
    a_p, tail_p = _conv_branch(xn_p, w_conv3, conv_w[l], 512)
    qa, qb, k32, kb, v32, vb = _qkv(xn_p, w_qkv, qg2 * (scale * LOG2E), kg2, 512)
    o_p = _attention(par_p, qa, qb, kb, vb, sub, 512)
    g_p = _gates(xn_p, w_g, 512, 1024)
    h_p = _mix(a_p, o_p, g_p, xp, w_out_conv[l].astype(bf16), w_out_attn[l].astype(bf16),
               w_out[l].astype(bf16), 256)

    xn_s = _rmsnorm(xs, norm_mix[l], f32, n_dec)
    z_s = _mm_small(xn_s, w_in_l, 1024)
    st = state_conv[l]
    a_s, u_s, q_s, k_s, g_s = _dec_pre(z_s, st[:, 0], st[:, 1], conv_w[l], qg2 * scale, kg2)
    v_s = z_s[:, 3 * W_CONV + 2 * ATTN_W:3 * W_CONV + 3 * ATTN_W]
    hd = lambda a: a.reshape(n_dec, N_HEADS, LANES)
    o_s = _dec_attention(page_table, par_s, slope_rep, hd(q_s), hd(k_s), hd(v_s), sub,
                         cache_k[l], cache_v[l], 8).reshape(n_dec, ATTN_W)
    m_s = _mm_gate_small(a_s, o_s, g_s, w_out_conv[l], w_out_attn[l], 512)
    h_s = _mm_res_small(xs, m_s, w_out[l], 512)

    tg = 256
    n_pairs = 2 * (t_p + n_dec)
    n_tiles = (n_pairs + N_EXPERTS * (tg - 1)) // tg + 1
    zeros = jnp.zeros((SUBLANES, LANES), f32)
    x2_p, idx_p, wt_p, cnt_p = _router(h_p, norm_ffn[l], w_router, b_router, zeros, 256, False)
    x2_s, idx_s, wt_s, cnt = _router(h_s, norm_ffn[l], w_router, b_router, cnt_p, n_dec, True)

    counts = cnt[0, N_GROUPS:N_GROUPS + N_EXPERTS].astype(i32)
    tiles_e = (counts + tg - 1) // tg
    tile_end = jnp.cumsum(tiles_e)
    tile_start = tile_end - tiles_e
    n_used = tile_end[-1]
    tid = jnp.arange(n_tiles, dtype=i32)
    t_exp = jnp.minimum(jnp.searchsorted(tile_end, tid, side="right"), N_EXPERTS - 1).astype(i32)
    last_e = t_exp[jnp.maximum(n_used - 1, 0)]
    t_exp = jnp.where(tid < n_used, t_exp, last_e)
    t_valid = jnp.clip(counts[t_exp] - (tid - tile_start[t_exp]) * tg, 0, tg)
    sched = jnp.concatenate([t_exp, t_valid, n_used[None]]).astype(i32)

    def slots(idx):
        return (tile_start[idx[:, 0:2]] * tg + idx[:, 2:4]).reshape(-1).astype(i32)

    pos_p = slots(idx_p)
    pos_s = slots(idx_s)
    xsort = _dispatch(pos_p, pos_s, x2_p, x2_s, n_tiles * tg, 512)
    y = _moe(sched, xsort, w_gate[l], w_up[l], w_down[l], tg, n_tiles)
    out_p = _combine(pos_p, h_p, wt_p, y, 256)
    out_s = _combine(pos_s, h_s, wt_s, y, n_dec)

    kv_shape_p = (depth, 1, t_p, N_HEADS, 2 * HEAD_DIM)
    kv_shape_s = (depth, n_dec, 1, N_HEADS, 2 * HEAD_DIM)
    conv_sample = jnp.stack([st[:, 1], u_s], axis=1)[None]
    return (out_p.reshape(1, t_p, D_MODEL), out_s.reshape(n_dec, 1, D_MODEL),
            k32.reshape(kv_shape_p), v32.reshape(kv_shape_p),
            tail_p[SUBLANES - (CONV_K - 1):].reshape(depth, 1, CONV_K - 1, W_CONV),
            k_s.reshape(kv_shape_s), v_s.reshape(kv_shape_s), conv_sample)
```
